```python
import math
import jax, jax.numpy as jnp
from jax import lax
import numpy as np

D_MODEL = 4096
BATCH = 2
SEQ = 4096
DEPTH = 1
DEC_BATCH = 8
DEC_SEQ = 2048
PAST_LEN = 128

GRID_W = 64
MIX_WIDTH = D_MODEL
ATTN_WIDTH = MIX_WIDTH // 2
FOURIER_WIDTH = MIX_WIDTH - ATTN_WIDTH
HEAD_DIM = 128
N_HEADS = ATTN_WIDTH // HEAD_DIM
N_KV_HEADS = N_HEADS // 4
KV_WIDTH = N_KV_HEADS * HEAD_DIM
N_FOURIER_GROUPS = 8
FOURIER_GROUP_DIM = FOURIER_WIDTH // N_FOURIER_GROUPS
ROPE_THETA = 10000.0
ROPE_AXIS_DIM = HEAD_DIM // 2
Q_BLOCK = 128
RMS_EPS = 1e-6
LN_EPS = 1e-5
DEEPNORM_ALPHA = (2.0 * DEPTH) ** 0.25
DEEPNORM_BETA = (8.0 * DEPTH) ** -0.25
IN_WIDTH = ATTN_WIDTH + 2 * KV_WIDTH + FOURIER_WIDTH + MIX_WIDTH
SPLITS = (ATTN_WIDTH, ATTN_WIDTH + KV_WIDTH, ATTN_WIDTH + 2 * KV_WIDTH,
          ATTN_WIDTH + 2 * KV_WIDTH + FOURIER_WIDTH)

kernel_name = "hymba_fnet_axial_gqa_encoder"


def _layernorm(x):
    xf = x.astype(jnp.float32)
    mu = jnp.mean(xf, axis=-1, keepdims=True)
    var = jnp.mean(jnp.square(xf - mu), axis=-1, keepdims=True)
    return ((xf - mu) * lax.rsqrt(var + LN_EPS)).astype(x.dtype)


def _rmsnorm(x, g):
    xf = x.astype(jnp.float32)
    y = xf * lax.rsqrt(jnp.mean(xf * xf, axis=-1, keepdims=True) + RMS_EPS)
    return (y * g.astype(jnp.float32)).astype(x.dtype)


def _axial_rope_tables(n_tokens):
    rows = n_tokens // GRID_W
    row_id = jnp.repeat(jnp.arange(rows, dtype=jnp.float32), GRID_W)
    col_id = jnp.tile(jnp.arange(GRID_W, dtype=jnp.float32), rows)
    inv_freq = ROPE_THETA ** (-jnp.arange(0, ROPE_AXIS_DIM, 2, dtype=jnp.float32) / ROPE_AXIS_DIM)
    ang_r = row_id[:, None] * inv_freq[None, :]
    ang_c = col_id[:, None] * inv_freq[None, :]
    return jnp.cos(ang_r), jnp.sin(ang_r), jnp.cos(ang_c), jnp.sin(ang_c)


def _rope_rotate(x, cos, sin):
    x1, x2 = jnp.split(x, 2, axis=-1)
    cos = cos[:, None, :].astype(x.dtype)
    sin = sin[:, None, :].astype(x.dtype)
    return jnp.concatenate([x1 * cos - x2 * sin, x1 * sin + x2 * cos], axis=-1)


def _apply_axial_rope(x, tables):
    cos_r, sin_r, cos_c, sin_c = tables
    x_row, x_col = jnp.split(x, 2, axis=-1)
    return jnp.concatenate([_rope_rotate(x_row, cos_r, sin_r), _rope_rotate(x_col, cos_c, sin_c)], axis=-1)


def _block_attention(q, k, v):
    b, s, h, d = q.shape
    n_blk = s // Q_BLOCK
    g = h // N_KV_HEADS
    qb = q.reshape(b, n_blk, Q_BLOCK, N_KV_HEADS, g, d).transpose(1, 0, 2, 3, 4, 5)
    scale = 1.0 / math.sqrt(d)

    def one_block(q_blk):
        scores = jnp.einsum('bqkgd,bskd->bkgqs', q_blk, k, preferred_element_type=jnp.float32) * scale
        p = jax.nn.softmax(scores, axis=-1).astype(v.dtype)
        return jnp.einsum('bkgqs,bskd->bqkgd', p, v)

    out = lax.map(one_block, qb)
    return out.transpose(1, 0, 2, 3, 4, 5).reshape(b, s, h * d)


def _fourier_mix(u, w_four):
    b, s, _ = u.shape
    ug = u.reshape(b, s, N_FOURIER_GROUPS, FOURIER_GROUP_DIM).astype(jnp.float32)
    f = jnp.real(jnp.fft.fft2(ug, axes=(1, 3), norm='ortho')).astype(u.dtype)
    y = jnp.einsum('bsgc,gcd->bsgd', f, w_four)
    return y.reshape(b, s, FOURIER_WIDTH)


def _encoder_layer(x, c, w_ada, b_ada, w_in, q_gain, k_gain, w_four, w_out, b_out, ln_g, ln_b):
    b, s, _ = x.shape
    mod = jnp.einsum('bd,de->be', jax.nn.silu(c), w_ada) + b_ada
    shift, scale, gate = jnp.split(mod, 3, axis=-1)
    h = _layernorm(x) * (1.0 + scale[:, None, :]) + shift[:, None, :]
    proj = jnp.einsum('bsd,de->bse', h, w_in)
    q, k, v, u_f, z = jnp.split(proj, SPLITS, axis=-1)
    q = q.reshape(b, s, N_HEADS, HEAD_DIM)
    k = k.reshape(b, s, N_KV_HEADS, HEAD_DIM)
    v = v.reshape(b, s, N_KV_HEADS, HEAD_DIM)
    tables = _axial_rope_tables(s)
    q = _apply_axial_rope(_rmsnorm(q, q_gain), tables)
    k = _apply_axial_rope(_rmsnorm(k, k_gain), tables)
    o_attn = _block_attention(q, k, v)
    o_four = _fourier_mix(u_f, w_four)
    o = jnp.concatenate([o_attn, o_four], axis=-1) * jax.nn.silu(z)
    y = jnp.einsum('bse,ed->bsd', o, w_out) + b_out
    res = DEEPNORM_ALPHA * x + gate[:, None, :] * y
    return _layernorm(res) * ln_g + ln_b


def setup_inputs(seed: int = 0) -> dict:
    key = jax.random.key(seed)
    ks = jax.random.split(key, 14)
    f32 = jnp.float32
    x_prompt = jax.random.normal(ks[0], (BATCH, SEQ, D_MODEL), f32)
    x_sample = jax.random.normal(ks[1], (DEC_BATCH, DEC_SEQ, D_MODEL), f32)
    c_prompt = jax.random.normal(ks[2], (BATCH, D_MODEL), f32)
    c_sample = jax.random.normal(ks[3], (DEC_BATCH, D_MODEL), f32)
    w_ada = jax.random.normal(ks[4], (DEPTH, D_MODEL, 3 * D_MODEL), f32) * (0.5 * D_MODEL ** -0.5)
    b_ada = jax.random.normal(ks[5], (DEPTH, 3 * D_MODEL), f32) * 0.02
    w_in = jax.random.normal(ks[6], (DEPTH, D_MODEL, IN_WIDTH), f32) * D_MODEL ** -0.5
    q_gain = 1.0 + 0.02 * jax.random.normal(ks[7], (DEPTH, HEAD_DIM), f32)
    k_gain = 1.0 + 0.02 * jax.random.normal(ks[8], (DEPTH, HEAD_DIM), f32)
    w_four = jax.random.normal(ks[9], (DEPTH, N_FOURIER_GROUPS, FOURIER_GROUP_DIM, FOURIER_GROUP_DIM), f32) * FOURIER_GROUP_DIM ** -0.5
    w_out = jax.random.normal(ks[10], (DEPTH, MIX_WIDTH, D_MODEL), f32) * (MIX_WIDTH ** -0.5 * DEEPNORM_BETA)
    b_out = jax.random.normal(ks[11], (DEPTH, D_MODEL), f32) * 0.02
    ln_g = 1.0 + 0.02 * jax.random.normal(ks[12], (DEPTH, D_MODEL), f32)
    ln_b = 0.02 * jax.random.normal(ks[13], (DEPTH, D_MODEL), f32)
    return {"x_prompt": x_prompt, "x_sample": x_sample, "c_prompt": c_prompt, "c_sample": c_sample,
            "w_ada": w_ada, "b_ada": b_ada, "w_in": w_in, "q_gain": q_gain, "k_gain": k_gain,
            "w_four": w_four, "w_out": w_out, "b_out": b_out, "ln_g": ln_g, "ln_b": ln_b}


def reference(x_prompt, x_sample, c_prompt, c_sample, w_ada, b_ada, w_in, q_gain, k_gain,
              w_four, w_out, b_out, ln_g, ln_b):
    y_prompt = x_prompt
    y_sample = x_sample
    for i in range(DEPTH):
        y_prompt = _encoder_layer(y_prompt, c_prompt, w_ada[i], b_ada[i], w_in[i], q_gain[i], k_gain[i],
                                  w_four[i], w_out[i], b_out[i], ln_g[i], ln_b[i])
        y_sample = _encoder_layer(y_sample, c_sample, w_ada[i], b_ada[i], w_in[i], q_gain[i], k_gain[i],
                                  w_four[i], w_out[i], b_out[i], ln_g[i], ln_b[i])
    return (y_prompt, y_sample)
```

```python
import functools
import math

import jax
import jax.numpy as jnp
from jax import lax
from jax.experimental import pallas as pl
from jax.experimental.pallas import tpu as pltpu

GRID_W = 64
HEAD_DIM = 128
N_HEADS = 16
N_KV_HEADS = 4
GQA_GROUP = N_HEADS // N_KV_HEADS
N_FOURIER_GROUPS = 8
FOURIER_GROUP_DIM = 256
ROPE_THETA = 10000.0
ROPE_AXIS_DIM = HEAD_DIM // 2
RMS_EPS = 1e-6
LN_EPS = 1e-5
DEPTH = 1
DEEPNORM_ALPHA = (2.0 * DEPTH) ** 0.25

ATTN_WIDTH = N_HEADS * HEAD_DIM
KV_WIDTH = N_KV_HEADS * HEAD_DIM
FOURIER_WIDTH = N_FOURIER_GROUPS * FOURIER_GROUP_DIM
MIX_WIDTH = ATTN_WIDTH + FOURIER_WIDTH
K_OFF = ATTN_WIDTH
V_OFF = K_OFF + KV_WIDTH
U_OFF = V_OFF + KV_WIDTH
Z_OFF = U_OFF + FOURIER_WIDTH
IN_WIDTH = Z_OFF + MIX_WIDTH

V7X_VMEM_BYTES = 64 * 1024 * 1024
MIB = 1024 * 1024

BF16 = jnp.bfloat16
F32 = jnp.float32


def _params(n_grid_dims, vmem_mib):
    return pltpu.CompilerParams(
        dimension_semantics=("arbitrary",) * n_grid_dims,
        vmem_limit_bytes=vmem_mib * MIB)


ADA_TN = 512


def _ada_kernel(c_ref, w_ref, b_ref, o_ref):
    c = c_ref[...]
    sc = (c * jax.nn.sigmoid(c)).astype(BF16)
    o_ref[...] = jnp.dot(sc, w_ref[...].astype(BF16),
                         preferred_element_type=F32) + b_ref[...]


def _ada_modulation(c_all, w_ada, b_ada):
    rows, d = c_all.shape
    n_out = w_ada.shape[1]
    return pl.pallas_call(
        _ada_kernel,
        grid=(n_out // ADA_TN,),
        in_specs=[
            pl.BlockSpec((rows, d), lambda n: (0, 0)),
            pl.BlockSpec((d, ADA_TN), lambda n: (0, n)),
            pl.BlockSpec((1, ADA_TN), lambda n: (0, n)),
        ],
        out_specs=pl.BlockSpec((rows, ADA_TN), lambda n: (0, n)),
        out_shape=jax.ShapeDtypeStruct((rows, n_out), F32),
        compiler_params=_params(1, 40),
        name="ada_modulation",
    )(c_all, w_ada, b_ada.reshape(1, n_out))


def _fold_kernel(cc_ref, sc_ref, w_ref, a_ref, b_ref):
    w = w_ref[0]
    a_ref[0] = jnp.dot(cc_ref[...], w, preferred_element_type=F32,
                       precision=lax.Precision.HIGHEST).astype(BF16)
    b_ref[0] = jnp.dot(sc_ref[...], w, preferred_element_type=F32,
                       precision=lax.Precision.HIGHEST).astype(BF16)


def _fold_fourier_weights(cos_c, sin_c, w_four):
    g, c, _ = w_four.shape
    mat = pl.BlockSpec((c, c), lambda i: (0, 0))
    grp = pl.BlockSpec((1, c, c), lambda i: (i, 0, 0))
    return pl.pallas_call(
        _fold_kernel,
        grid=(g,),
        in_specs=[mat, mat, grp],
        out_specs=[grp, grp],
        out_shape=[jax.ShapeDtypeStruct((g, c, c), BF16)] * 2,
        compiler_params=_params(1, 16),
        name="fold_fourier_weights",
    )(cos_c, sin_c, w_four)


IN_TM = 512
IN_TN = 1024
Q_TILES = ATTN_WIDTH // IN_TN
KV_TILE = K_OFF // IN_TN
Z_TILE0 = Z_OFF // IN_TN
assert K_OFF % IN_TN == 0 and V_OFF + KV_WIDTH == (KV_TILE + 1) * IN_TN
assert Z_OFF % IN_TN == 0


def _rms_rope(x, gain, cos, sin, first_half):
    ms = jnp.mean(x * x, axis=-1, keepdims=True)
    y = x * lax.rsqrt(ms + RMS_EPS) * gain
    partner = jnp.where(first_half,
                        pltpu.roll(y, HEAD_DIM - ROPE_AXIS_DIM // 2, 1),
                        pltpu.roll(y, ROPE_AXIS_DIM // 2, 1))
    return y * cos + partner * sin


def _in_proj_kernel(x_ref, shift_ref, scale_ref, w_ref, cos_ref, sin_ref,
                    qg_ref, kg_ref, o_ref, h_ref):
    n = pl.program_id(1)

    @pl.when(n == 0)
    def _():
        x = x_ref[...]
        mu = jnp.mean(x, axis=-1, keepdims=True)
        xc = x - mu
        var = jnp.mean(xc * xc, axis=-1, keepdims=True)
        hn = xc * lax.rsqrt(var + LN_EPS)
        h_ref[...] = (hn * (1.0 + scale_ref[...]) + shift_ref[...]).astype(BF16)

    acc = jnp.dot(h_ref[...], w_ref[...], preferred_element_type=F32)

    lane = lax.broadcasted_iota(jnp.int32, (IN_TM, HEAD_DIM), 1)
    first_half = (lane % ROPE_AXIS_DIM) < (ROPE_AXIS_DIM // 2)

    @pl.when(n < Q_TILES)
    def _():
        cos = cos_ref[...]
        sin = sin_ref[...]
        gain = qg_ref[...]
        q_scale = 1.0 / math.sqrt(HEAD_DIM)
        for hh in range(IN_TN // HEAD_DIM):
            sl = slice(hh * HEAD_DIM, (hh + 1) * HEAD_DIM)
            o_ref[:, sl] = (_rms_rope(acc[:, sl], gain, cos, sin, first_half)
                            * q_scale).astype(BF16)

    @pl.when(n == KV_TILE)
    def _():
        cos = cos_ref[...]
        sin = sin_ref[...]
        gain = kg_ref[...]
        for hh in range(N_KV_HEADS):
            sl = slice(hh * HEAD_DIM, (hh + 1) * HEAD_DIM)
            o_ref[:, sl] = _rms_rope(acc[:, sl], gain, cos, sin,
                                     first_half).astype(BF16)
        o_ref[:, KV_WIDTH:] = acc[:, KV_WIDTH:].astype(BF16)

    @pl.when(jnp.logical_and(n > KV_TILE, n < Z_TILE0))
    def _():
        o_ref[...] = acc.astype(BF16)

    @pl.when(n >= Z_TILE0)
    def _():
        o_ref[...] = (acc * jax.nn.sigmoid(acc)).astype(BF16)


def _in_projection(x2, shift, scale, w_in_bf, cos_t, sin_t, q_gain, k_gain, seq):
    m, d = x2.shape
    blocks_per_seq = seq // IN_TM
    vec = pl.BlockSpec((None, 1, d), lambda i, n: (i // blocks_per_seq, 0, 0))
    tab = pl.BlockSpec((IN_TM, HEAD_DIM), lambda i, n: (i % blocks_per_seq, 0))
    gain = pl.BlockSpec((1, HEAD_DIM), lambda i, n: (0, 0))
    return pl.pallas_call(
        _in_proj_kernel,
        grid=(m // IN_TM, IN_WIDTH // IN_TN),
        in_specs=[
            pl.BlockSpec((IN_TM, d), lambda i, n: (i, 0)),
            vec, vec,
            pl.BlockSpec((d, IN_TN), lambda i, n: (0, n)),
            tab, tab, gain, gain,
        ],
        out_specs=pl.BlockSpec((IN_TM, IN_TN), lambda i, n: (i, n)),
        out_shape=jax.ShapeDtypeStruct((m, IN_WIDTH), BF16),
        scratch_shapes=[pltpu.VMEM((IN_TM, d), BF16)],
        compiler_params=_params(2, 56),
        name="in_projection",
    )(x2, shift, scale, w_in_bf, cos_t, sin_t, q_gain, k_gain)


ATTN_BQ = 256


def _attn_kernel(q_ref, k_ref, v_ref, g_ref, o_ref):
    k = k_ref[...]
    v = v_ref[...]
    for g in range(GQA_GROUP):
        sl = slice(g * HEAD_DIM, (g + 1) * HEAD_DIM)
        s = lax.dot_general(q_ref[:, sl], k, (((1,), (1,)), ((), ())),
                            preferred_element_type=F32)
        mx = jnp.max(s, axis=-1, keepdims=True)
        p = jnp.exp(s - mx)
        denom = jnp.sum(p, axis=-1, keepdims=True)
        og = jnp.dot(p.astype(BF16), v, preferred_element_type=F32) / denom
        o_ref[:, sl] = (og * g_ref[:, sl].astype(F32)).astype(BF16)


def _attention(proj3, o_shape):
    b, s, _ = proj3.shape
    qw = GQA_GROUP * HEAD_DIM
    return pl.pallas_call(
        _attn_kernel,
        grid=(b, N_KV_HEADS, s // ATTN_BQ),
        in_specs=[
            pl.BlockSpec((None, ATTN_BQ, qw), lambda bi, h, i: (bi, i, h)),
            pl.BlockSpec((None, s, HEAD_DIM),
                         lambda bi, h, i: (bi, 0, K_OFF // HEAD_DIM + h)),
            pl.BlockSpec((None, s, HEAD_DIM),
                         lambda bi, h, i: (bi, 0, V_OFF // HEAD_DIM + h)),
            pl.BlockSpec((None, ATTN_BQ, qw),
                         lambda bi, h, i: (bi, i, Z_OFF // qw + h)),
        ],
        out_specs=pl.BlockSpec((None, ATTN_BQ, qw), lambda bi, h, i: (bi, i, h)),
        out_shape=o_shape,
        compiler_params=_params(3, 48),
        name="gqa_attention",
    )(proj3, proj3, proj3, proj3)


FOUR_TM = 512
FOUR_TN = 1024
FOUR_GROUPS_PER_TILE = FOUR_TN // FOURIER_GROUP_DIM


def _fourier_kernel(c_ref, s_ref, u_ref, a_ref, b_ref, g_ref, oa_ref, o_ref,
                    *, inv_norm):
    del oa_ref
    u = u_ref[...]
    rc = jnp.dot(c_ref[...], u, preferred_element_type=F32).astype(BF16)
    rs = jnp.dot(s_ref[...], u, preferred_element_type=F32).astype(BF16)
    for gi in range(FOUR_GROUPS_PER_TILE):
        sl = slice(gi * FOURIER_GROUP_DIM, (gi + 1) * FOURIER_GROUP_DIM)
        t = (jnp.dot(rc[:, sl], a_ref[gi], preferred_element_type=F32)
             - jnp.dot(rs[:, sl], b_ref[gi], preferred_element_type=F32))
        o_ref[:, sl] = (t * inv_norm * g_ref[:, sl].astype(F32)).astype(BF16)


def _fourier_mix(proj3, o_attn, dft_cos, dft_sin, a_f, b_f):
    b, s, _ = proj3.shape
    tab = pl.BlockSpec((FOUR_TM, s), lambda bi, n, i: (i, 0))
    wts = pl.BlockSpec((FOUR_GROUPS_PER_TILE, FOURIER_GROUP_DIM, FOURIER_GROUP_DIM),
                       lambda bi, n, i: (n, 0, 0))
    out_col0 = ATTN_WIDTH // FOUR_TN
    return pl.pallas_call(
        functools.partial(_fourier_kernel, inv_norm=1.0 / math.sqrt(s)),
        grid=(b, FOURIER_WIDTH // FOUR_TN, s // FOUR_TM),
        in_specs=[
            tab, tab,
            pl.BlockSpec((None, s, FOUR_TN),
                         lambda bi, n, i: (bi, 0, U_OFF // FOUR_TN + n)),
            wts, wts,
            pl.BlockSpec((None, FOUR_TM, FOUR_TN),
                         lambda bi, n, i: (bi, i, (Z_OFF + ATTN_WIDTH) // FOUR_TN + n)),
            pl.BlockSpec(memory_space=pl.ANY),
        ],
        out_specs=pl.BlockSpec((None, FOUR_TM, FOUR_TN),
                               lambda bi, n, i: (bi, i, out_col0 + n)),
        out_shape=jax.ShapeDtypeStruct(o_attn.shape, o_attn.dtype),
        input_output_aliases={6: 0},
        compiler_params=_params(3, 56),
        name="fourier_mix",
    )(dft_cos, dft_sin, proj3, a_f, b_f, proj3, o_attn)


OUT_TM = 256
OUT_TN = 1024


def _out_proj_kernel(o_ref, w_ref, x_ref, gate_ref, bo_ref, lg_ref, lb_ref,
                     out_ref, y_ref):
    n = pl.program_id(1)
    n_tiles = pl.num_programs(1)
    y_ref[n] = jnp.dot(o_ref[...], w_ref[...], preferred_element_type=F32)

    @pl.when(n == n_tiles - 1)
    def _():
        d = out_ref.shape[-1]
        tiles = d // OUT_TN
        total = jnp.zeros((OUT_TM, 1), F32)
        for j in range(tiles):
            sl = slice(j * OUT_TN, (j + 1) * OUT_TN)
            r = (DEEPNORM_ALPHA * x_ref[:, sl]
                 + gate_ref[:, sl] * (y_ref[j] + bo_ref[:, sl]))
            y_ref[j] = r
            total = total + jnp.sum(r, axis=-1, keepdims=True)
        mu = total / d
        sq = jnp.zeros((OUT_TM, 1), F32)
        for j in range(tiles):
            rc = y_ref[j] - mu
            sq = sq + jnp.sum(rc * rc, axis=-1, keepdims=True)
        inv = lax.rsqrt(sq / d + LN_EPS)
        for j in range(tiles):
            sl = slice(j * OUT_TN, (j + 1) * OUT_TN)
            out_ref[:, sl] = (y_ref[j] - mu) * inv * lg_ref[:, sl] + lb_ref[:, sl]


def _out_projection(o2, w_out_bf, x2, gate, b_out, ln_g, ln_b, seq):
    m, d = x2.shape
    e = o2.shape[1]
    blocks_per_seq = seq // OUT_TM
    row = pl.BlockSpec((1, d), lambda i, n: (0, 0))
    return pl.pallas_call(
        _out_proj_kernel,
        grid=(m // OUT_TM, d // OUT_TN),
        in_specs=[
            pl.BlockSpec((OUT_TM, e), lambda i, n: (i, 0)),
            pl.BlockSpec((e, OUT_TN), lambda i, n: (0, n)),
            pl.BlockSpec((OUT_TM, d), lambda i, n: (i, 0)),
            pl.BlockSpec((None, 1, d), lambda i, n: (i // blocks_per_seq, 0, 0)),
            row, row, row,
        ],
        out_specs=pl.BlockSpec((OUT_TM, d), lambda i, n: (i, 0)),
        out_shape=jax.ShapeDtypeStruct((m, d), F32),
        scratch_shapes=[pltpu.VMEM((d // OUT_TN, OUT_TM, OUT_TN), F32)],
        compiler_params=_params(2, 48),
        name="out_projection",
    )(o2, w_out_bf, x2, gate, b_out, ln_g, ln_b)


def _rope_tables(seq):
    rows = seq // GRID_W
    row_id = jnp.repeat(jnp.arange(rows, dtype=F32), GRID_W)
    col_id = jnp.tile(jnp.arange(GRID_W, dtype=F32), rows)
    inv_freq = ROPE_THETA ** (-jnp.arange(0, ROPE_AXIS_DIM, 2, dtype=F32) / ROPE_AXIS_DIM)
    ang_r = row_id[:, None] * inv_freq[None, :]
    ang_c = col_id[:, None] * inv_freq[None, :]
    cos_t = jnp.concatenate([jnp.cos(ang_r)] * 2 + [jnp.cos(ang_c)] * 2, axis=-1)
    sin_t = jnp.concatenate([-jnp.sin(ang_r), jnp.sin(ang_r),
                             -jnp.sin(ang_c), jnp.sin(ang_c)], axis=-1)
    return cos_t, sin_t


def _dft_tables(n, dtype):
    idx = jnp.arange(n, dtype=jnp.int32)
    phase = (idx[:, None] * idx[None, :]) % n
    ang = phase.astype(F32) * (2.0 * math.pi / n)
    return jnp.cos(ang).astype(dtype), jnp.sin(ang).astype(dtype)


def _encoder_group(x, mod, w_in_bf, w_out_bf, q_gain, k_gain, a_f, b_f,
                   b_out, ln_g, ln_b):
    b, s, d = x.shape
    x2 = x.reshape(b * s, d)
    shift = mod[:, 0 * d:1 * d].reshape(b, 1, d)
    scale = mod[:, 1 * d:2 * d].reshape(b, 1, d)
    gate = mod[:, 2 * d:3 * d].reshape(b, 1, d)
    cos_t, sin_t = _rope_tables(s)
    dft_cos, dft_sin = _dft_tables(s, BF16)

    proj = _in_projection(x2, shift, scale, w_in_bf, cos_t, sin_t,
                          q_gain, k_gain, s)
    proj3 = proj.reshape(b, s, IN_WIDTH)
    o = _attention(proj3, jax.ShapeDtypeStruct((b, s, MIX_WIDTH), BF16))
    o = _fourier_mix(proj3, o, dft_cos, dft_sin, a_f, b_f)
    y = _out_projection(o.reshape(b * s, MIX_WIDTH), w_out_bf, x2, gate,
                        b_out, ln_g, ln_b, s)
    return y.reshape(b, s, d)


def kernel(x_prompt, x_sample, c_prompt, c_sample, w_ada, b_ada, w_in, q_gain,
           k_gain, w_four, w_out, b_out, ln_g, ln_b):
    assert w_ada.shape[0] == DEPTH
    d = x_prompt.shape[-1]
    bp = c_prompt.shape[0]
    bs = c_sample.shape[0]
    pad = (-(bp + bs)) % 8
    c_all = jnp.concatenate([c_prompt, c_sample, jnp.zeros((pad, d), F32)], axis=0)

    cos_c, sin_c = _dft_tables(FOURIER_GROUP_DIM, F32)
    ch_norm = 1.0 / math.sqrt(FOURIER_GROUP_DIM)

    y_p, y_s = x_prompt, x_sample
    for i in range(DEPTH):
        mod = _ada_modulation(c_all, w_ada[i], b_ada[i])
        a_f, b_f = _fold_fourier_weights(cos_c * ch_norm, sin_c * ch_norm, w_four[i])
        w_in_bf = w_in[i].astype(BF16)
        w_out_bf = w_out[i].astype(BF16)
        args = (w_in_bf, w_out_bf, q_gain[i].reshape(1, HEAD_DIM),
                k_gain[i].reshape(1, HEAD_DIM), a_f, b_f,
                b_out[i].reshape(1, d), ln_g[i].reshape(1, d), ln_b[i].reshape(1, d))
        y_p = _encoder_group(y_p, mod[:bp], *args)
        y_s = _encoder_group(y_s, mod[bp:bp + bs], *args)
    return (y_p, y_s)
```
